```python
import math, functools
import jax, jax.numpy as jnp
from jax import lax
import numpy as np

D_MODEL = 2048
BATCH = 4
SEQ = 8192
DEPTH = 1
DEC_BATCH = 8
DEC_SEQ = 16
PAST_LEN = 2048

CHUNK = 64
PAST_CHUNKS = 8
BAND = (PAST_CHUNKS + 1) * CHUNK
REL_CLIP = 128
N_ATT_HEADS = 12
ATT_HEAD_DIM = 128
ATT_WIDTH = N_ATT_HEADS * ATT_HEAD_DIM
RNN_BLOCKS = 12
RNN_BLOCK_DIM = 128
RNN_WIDTH = RNN_BLOCKS * RNN_BLOCK_DIM
CONV_WIDTH = 4
LRU_C = 8.0
N_MEM = 256
N_MEM_HEADS = 4
MEM_HEAD_DIM = 256
MEM_WIDTH = N_MEM_HEADS * MEM_HEAD_DIM
N_BRANCH = 3
N_EXPERTS = 32
TOP_K = 4
D_FF = D_MODEL
SWIGLU_LIMIT = 7.0
SWIGLU_ALPHA = 1.702
MOE_BLOCK = 128
LN_EPS = 1e-5
DN_ALPHA = (2.0 * DEPTH) ** 0.25
DN_BETA = (8.0 * DEPTH) ** -0.25
OFF_RX = 0
OFF_RG = OFF_RX + RNN_WIDTH
OFF_Q = OFF_RG + RNN_WIDTH
OFF_K = OFF_Q + ATT_WIDTH
OFF_V = OFF_K + ATT_WIDTH
OFF_QM = OFF_V + ATT_WIDTH
IN_WIDTH = OFF_QM + MEM_WIDTH

kernel_name = 'hybrid_streaming_encoder_step'


def _att_rows(n_frames):
    return min(PAST_CHUNKS * CHUNK, n_frames)


def _layer_norm(x, g, b):
    xf = x.astype(jnp.float32)
    mu = jnp.mean(xf, -1, keepdims=True)
    var = jnp.mean(jnp.square(xf - mu), -1, keepdims=True)
    y = (xf - mu) * lax.rsqrt(var + LN_EPS) * g.astype(jnp.float32) + b.astype(jnp.float32)
    return y.astype(x.dtype)


def _rel_bias(rel_bias, dist):
    idx = jnp.clip(dist, -REL_CLIP, REL_CLIP) + REL_CLIP
    return rel_bias[:, idx].astype(jnp.float32)


def _softmax_attend(q, k, v, bias=None, valid=None):
    s = jnp.einsum('bqhd,bkhd->bhqk', q, k).astype(jnp.float32) * (q.shape[-1] ** -0.5)
    if bias is not None:
        s = s + bias
    if valid is not None:
        s = jnp.where(valid, s, -1e30)
    p = jax.nn.softmax(s, axis=-1).astype(v.dtype)
    return jnp.einsum('bhqk,bkhd->bqhd', p, v)


def _band_attention(q, k, v, rel_bias):
    B, S, H, Dh = q.shape
    n_chunks = S // CHUNK
    pad = PAST_CHUNKS * CHUNK
    kp = jnp.pad(k, ((0, 0), (pad, 0), (0, 0), (0, 0)))
    vp = jnp.pad(v, ((0, 0), (pad, 0), (0, 0), (0, 0)))
    dist = jnp.arange(CHUNK)[:, None] + pad - jnp.arange(BAND)[None, :]
    bias = _rel_bias(rel_bias, dist)

    def one_chunk(c):
        start = c * CHUNK
        qc = lax.dynamic_slice_in_dim(q, start, CHUNK, axis=1)
        kc = lax.dynamic_slice_in_dim(kp, start, BAND, axis=1)
        vc = lax.dynamic_slice_in_dim(vp, start, BAND, axis=1)
        valid = (start - pad + jnp.arange(BAND)) >= 0
        return _softmax_attend(qc, kc, vc, bias, valid)

    o = lax.map(one_chunk, jnp.arange(n_chunks))
    return jnp.swapaxes(o, 0, 1).reshape(B, S, H, Dh)


def _cached_band_attention(q, k, v, cache_k, cache_v, rel_bias):
    T = q.shape[1]
    R = cache_k.shape[1]
    kk = jnp.concatenate([cache_k.astype(k.dtype), k], axis=1)
    vv = jnp.concatenate([cache_v.astype(v.dtype), v], axis=1)
    dist = jnp.arange(T)[:, None] + R - jnp.arange(R + T)[None, :]
    return _softmax_attend(q, kk, vv, _rel_bias(rel_bias, dist))


def _causal_conv(u, buf, conv_w, conv_b):
    T = u.shape[1]
    ext = jnp.concatenate([buf.astype(u.dtype), u], axis=1)
    y = conv_b
    for tap in range(CONV_WIDTH):
        y = y + ext[:, tap:tap + T] * conv_w[tap]
    return y, ext[:, T:]


def _rglru(u, h0, lru_wa, lru_ba, lru_wx, lru_bx, lru_lambda):
    B, T, C = u.shape
    ub = u.reshape(B, T, RNN_BLOCKS, RNN_BLOCK_DIM)
    r = jax.nn.sigmoid((jnp.einsum('btnd,nde->btne', ub, lru_wa).reshape(B, T, C) + lru_ba).astype(jnp.float32))
    i = jax.nn.sigmoid((jnp.einsum('btnd,nde->btne', ub, lru_wx).reshape(B, T, C) + lru_bx).astype(jnp.float32))
    log_a = -LRU_C * r * jax.nn.softplus(-lru_lambda.astype(jnp.float32))
    a = jnp.exp(log_a)
    b = jnp.sqrt(-jnp.expm1(2.0 * log_a)) * (i * u.astype(jnp.float32))
    b = b.at[:, 0].add(a[:, 0] * h0.astype(jnp.float32))

    def combine(left, right):
        a_l, b_l = left
        a_r, b_r = right
        return a_l * a_r, a_r * b_l + b_r

    _, h = lax.associative_scan(combine, (a, b), axis=1)
    return h.astype(u.dtype), h[:, -1].astype(u.dtype)


def _moe(x, w_router, b_router, w_moe_gate, b_moe_gate, w_moe_up, b_moe_up, w_moe_down, b_moe_down):
    B, T, D = x.shape
    n_tok = B * T
    xf = x.reshape(n_tok, D)
    logits = (xf @ w_router + b_router).astype(jnp.float32)
    top_v, top_e = lax.top_k(logits, TOP_K)
    gates = jax.nn.softmax(top_v, axis=-1)
    flat_e = top_e.reshape(-1)
    n_assign = n_tok * TOP_K
    order = jnp.argsort(flat_e)
    sorted_e = flat_e[order]
    sorted_tok = (order // TOP_K).astype(jnp.int32)
    sorted_gate = gates.reshape(-1)[order]
    counts = jnp.bincount(flat_e, length=N_EXPERTS)
    padded = (counts + MOE_BLOCK - 1) // MOE_BLOCK * MOE_BLOCK
    pad_end = jnp.cumsum(padded)
    pad_start = pad_end - padded
    grp_start = jnp.cumsum(counts) - counts
    dest = pad_start[sorted_e] + jnp.arange(n_assign) - grp_start[sorted_e]
    n_blocks = -(-(n_assign + N_EXPERTS * (MOE_BLOCK - 1)) // MOE_BLOCK)
    n_rows = n_blocks * MOE_BLOCK
    row_tok = jnp.full((n_rows,), n_tok, jnp.int32).at[dest].set(sorted_tok)
    row_gate = jnp.zeros((n_rows,), jnp.float32).at[dest].set(sorted_gate)
    block_e = jnp.minimum(jnp.searchsorted(pad_end, jnp.arange(n_blocks) * MOE_BLOCK, side='right'), N_EXPERTS - 1)
    x_pad = jnp.concatenate([xf, jnp.zeros((1, D), xf.dtype)], axis=0)

    def expert_block(args):
        rows, e = args
        xb = x_pad[rows]
        g = jnp.minimum(xb @ w_moe_gate[e] + b_moe_gate[e], SWIGLU_LIMIT)
        u = jnp.clip(xb @ w_moe_up[e] + b_moe_up[e], -SWIGLU_LIMIT, SWIGLU_LIMIT)
        h = (u + 1.0) * (g * jax.nn.sigmoid(SWIGLU_ALPHA * g))
        return h @ w_moe_down[e] + b_moe_down[e]

    ys = lax.map(expert_block, (row_tok.reshape(n_blocks, MOE_BLOCK), block_e)).reshape(n_rows, D)
    out = jax.ops.segment_sum(ys * row_gate[:, None].astype(ys.dtype), row_tok, num_segments=n_tok + 1)
    return out[:n_tok].reshape(B, T, D)


def _trunk_layer(x, conv_buf, h0, attend, mem_k, mem_v, p):
    B, T, _ = x.shape
    z = x @ p['w_in']
    u = z[..., OFF_RX:OFF_RG]
    rg = z[..., OFF_RG:OFF_Q]
    q = z[..., OFF_Q:OFF_K].reshape(B, T, N_ATT_HEADS, ATT_HEAD_DIM)
    k = z[..., OFF_K:OFF_V].reshape(B, T, N_ATT_HEADS, ATT_HEAD_DIM)
    v = z[..., OFF_V:OFF_QM].reshape(B, T, N_ATT_HEADS, ATT_HEAD_DIM)
    qm = z[..., OFF_QM:IN_WIDTH].reshape(B, T, N_MEM_HEADS, MEM_HEAD_DIM)
    uc, new_buf = _causal_conv(u, conv_buf, p['conv_w'], p['conv_b'])
    h, h_last = _rglru(uc, h0, p['lru_wa'], p['lru_ba'], p['lru_wx'], p['lru_bx'], p['lru_lambda'])
    out_rnn = (h * jax.nn.gelu(rg)) @ p['w_proj_rnn']
    out_att = attend(q, k, v).reshape(B, T, ATT_WIDTH) @ p['w_proj_att']
    out_mem = _softmax_attend(qm, mem_k.astype(qm.dtype), mem_v.astype(qm.dtype)).reshape(B, T, MEM_WIDTH) @ p['w_proj_mem']
    gate = jax.nn.sigmoid(x @ p['w_gate'] + p['b_gate']).reshape(B, T, N_BRANCH, D_MODEL)
    mixed = gate[:, :, 0] * out_rnn + gate[:, :, 1] * out_att + gate[:, :, 2] * out_mem
    x1 = _layer_norm(DN_ALPHA * x + mixed @ p['w_out'], p['ln1_g'], p['ln1_b'])
    ffn = _moe(x1, p['w_router'], p['b_router'], p['w_moe_gate'], p['b_moe_gate'], p['w_moe_up'], p['b_moe_up'], p['w_moe_down'], p['b_moe_down'])
    x2 = _layer_norm(DN_ALPHA * x1 + ffn, p['ln2_g'], p['ln2_b'])
    return x2, new_buf, h_last, k, v


def setup_inputs(seed: int = 0) -> dict:
    key = jax.random.key(seed)
    ks = jax.random.split(key, 40)
    L = DEPTH
    rows = _att_rows(PAST_LEN)

    def nrm(i, shape, scale):
        return scale * jax.random.normal(ks[i], shape, jnp.float32)

    a8 = jax.random.uniform(ks[17], (L, RNN_WIDTH), jnp.float32, 0.9, 0.999)
    a = a8 ** (1.0 / LRU_C)
    lru_lambda = jnp.log(a) - jnp.log1p(-a)
    return {
        'x_prompt': nrm(0, (BATCH, SEQ, D_MODEL), 1.0),
        'x_sample': nrm(1, (DEC_BATCH, DEC_SEQ, D_MODEL), 1.0),
        'cache_att_k': nrm(2, (L, DEC_BATCH, rows, N_ATT_HEADS, ATT_HEAD_DIM), 1.0),
        'cache_att_v': nrm(3, (L, DEC_BATCH, rows, N_ATT_HEADS, ATT_HEAD_DIM), 1.0),
        'cache_mem_k': nrm(4, (L, DEC_BATCH, N_MEM, N_MEM_HEADS, MEM_HEAD_DIM), 1.0),
        'cache_mem_v': nrm(5, (L, DEC_BATCH, N_MEM, N_MEM_HEADS, MEM_HEAD_DIM), 1.0),
        'state_conv': nrm(6, (L, DEC_BATCH, CONV_WIDTH - 1, RNN_WIDTH), 1.0),
        'state_rglru': nrm(7, (L, DEC_BATCH, RNN_WIDTH), 0.5),
        'mem_prompt': nrm(8, (BATCH, N_MEM, D_MODEL), 1.0),
        'w_in': nrm(9, (L, D_MODEL, IN_WIDTH), D_MODEL ** -0.5),
        'conv_w': nrm(10, (L, CONV_WIDTH, RNN_WIDTH), CONV_WIDTH ** -0.5),
        'conv_b': nrm(11, (L, RNN_WIDTH), 0.01),
        'lru_wa': nrm(12, (L, RNN_BLOCKS, RNN_BLOCK_DIM, RNN_BLOCK_DIM), RNN_BLOCK_DIM ** -0.5),
        'lru_ba': nrm(13, (L, RNN_WIDTH), 0.01),
        'lru_wx': nrm(14, (L, RNN_BLOCKS, RNN_BLOCK_DIM, RNN_BLOCK_DIM), RNN_BLOCK_DIM ** -0.5),
        'lru_bx': nrm(15, (L, RNN_WIDTH), 0.01),
        'lru_lambda': lru_lambda,
        'rel_bias': nrm(16, (L, N_ATT_HEADS, 2 * REL_CLIP + 1), 0.1),
        'w_mem_kv': nrm(18, (L, D_MODEL, 2 * MEM_WIDTH), D_MODEL ** -0.5),
        'w_proj_rnn': nrm(19, (L, RNN_WIDTH, D_MODEL), RNN_WIDTH ** -0.5),
        'w_proj_att': nrm(20, (L, ATT_WIDTH, D_MODEL), ATT_WIDTH ** -0.5),
        'w_proj_mem': nrm(21, (L, MEM_WIDTH, D_MODEL), MEM_WIDTH ** -0.5),
        'w_gate': nrm(22, (L, D_MODEL, N_BRANCH * D_MODEL), D_MODEL ** -0.5),
        'b_gate': nrm(23, (L, N_BRANCH * D_MODEL), 0.01),
        'w_out': nrm(24, (L, D_MODEL, D_MODEL), DN_BETA * D_MODEL ** -0.5),
        'ln1_g': 1.0 + nrm(25, (L, D_MODEL), 0.01),
        'ln1_b': nrm(26, (L, D_MODEL), 0.01),
        'w_router': nrm(27, (L, D_MODEL, N_EXPERTS), D_MODEL ** -0.5),
        'b_router': nrm(28, (L, N_EXPERTS), 0.01),
        'w_moe_gate': nrm(29, (L, N_EXPERTS, D_MODEL, D_FF), D_MODEL ** -0.5),
        'b_moe_gate': nrm(30, (L, N_EXPERTS, D_FF), 0.01),
        'w_moe_up': nrm(31, (L, N_EXPERTS, D_MODEL, D_FF), D_MODEL ** -0.5),
        'b_moe_up': nrm(32, (L, N_EXPERTS, D_FF), 0.01),
        'w_moe_down': nrm(33, (L, N_EXPERTS, D_FF, D_MODEL), DN_BETA * D_FF ** -0.5),
        'b_moe_down': nrm(34, (L, N_EXPERTS, D_MODEL), 0.01),
        'ln2_g': 1.0 + nrm(35, (L, D_MODEL), 0.01),
        'ln2_b': nrm(36, (L, D_MODEL), 0.01),
    }


def reference(x_prompt, x_sample, cache_att_k, cache_att_v, cache_mem_k, cache_mem_v, state_conv, state_rglru, mem_prompt,
              w_in, conv_w, conv_b, lru_wa, lru_ba, lru_wx, lru_bx, lru_lambda, rel_bias, w_mem_kv,
              w_proj_rnn, w_proj_att, w_proj_mem, w_gate, b_gate, w_out, ln1_g, ln1_b,
              w_router, b_router, w_moe_gate, b_moe_gate, w_moe_up, b_moe_up, w_moe_down, b_moe_down, ln2_g, ln2_b):
    weights = dict(w_in=w_in, conv_w=conv_w, conv_b=conv_b, lru_wa=lru_wa, lru_ba=lru_ba, lru_wx=lru_wx, lru_bx=lru_bx,
                   lru_lambda=lru_lambda, rel_bias=rel_bias, w_mem_kv=w_mem_kv, w_proj_rnn=w_proj_rnn,
                   w_proj_att=w_proj_att, w_proj_mem=w_proj_mem, w_gate=w_gate, b_gate=b_gate, w_out=w_out,
                   ln1_g=ln1_g, ln1_b=ln1_b, w_router=w_router, b_router=b_router, w_moe_gate=w_moe_gate,
                   b_moe_gate=b_moe_gate, w_moe_up=w_moe_up, b_moe_up=b_moe_up, w_moe_down=w_moe_down,
                   b_moe_down=b_moe_down, ln2_g=ln2_g, ln2_b=ln2_b)
    n_prompt = x_prompt.shape[0]
    keep = _att_rows(x_prompt.shape[1])
    xp = x_prompt
    xs = x_sample
    att_k_p, att_v_p, mem_k_p, mem_v_p, conv_p, rglru_p = [], [], [], [], [], []
    att_k_s, att_v_s, conv_s, rglru_s = [], [], [], []
    for layer in range(DEPTH):
        pl = {name: w[layer] for name, w in weights.items()}
        mem_kv = mem_prompt @ pl['w_mem_kv']
        mk = mem_kv[..., :MEM_WIDTH].reshape(n_prompt, N_MEM, N_MEM_HEADS, MEM_HEAD_DIM)
        mv = mem_kv[..., MEM_WIDTH:].reshape(n_prompt, N_MEM, N_MEM_HEADS, MEM_HEAD_DIM)
        conv0 = jnp.zeros((n_prompt, CONV_WIDTH - 1, RNN_WIDTH), xp.dtype)
        h0 = jnp.zeros((n_prompt, RNN_WIDTH), xp.dtype)
        xp, cb_p, hl_p, k_p, v_p = _trunk_layer(
            xp, conv0, h0, functools.partial(_band_attention, rel_bias=pl['rel_bias']), mk, mv, pl)
        xs, cb_s, hl_s, k_s, v_s = _trunk_layer(
            xs, state_conv[layer], state_rglru[layer],
            functools.partial(_cached_band_attention, cache_k=cache_att_k[layer], cache_v=cache_att_v[layer], rel_bias=pl['rel_bias']),
            cache_mem_k[layer], cache_mem_v[layer], pl)
        att_k_p.append(k_p[:, -keep:])
        att_v_p.append(v_p[:, -keep:])
        mem_k_p.append(mk)
        mem_v_p.append(mv)
        conv_p.append(cb_p)
        rglru_p.append(hl_p)
        att_k_s.append(k_s)
        att_v_s.append(v_s)
        conv_s.append(cb_s)
        rglru_s.append(hl_s)
    return (xp, xs, jnp.stack(att_k_p), jnp.stack(att_v_p), jnp.stack(mem_k_p), jnp.stack(mem_v_p),
            jnp.stack(conv_p), jnp.stack(rglru_p), jnp.stack(att_k_s), jnp.stack(att_v_s),
            jnp.stack(conv_s), jnp.stack(rglru_s))
```

```python
import functools
import math

import jax
import jax.numpy as jnp
from jax import lax
from jax.experimental import pallas as pl
from jax.experimental.pallas import tpu as pltpu

F32 = jnp.float32
BF16 = jnp.bfloat16

D_MODEL = 2048
DEPTH = 1
CHUNK = 64
PAST_CHUNKS = 8
REL_CLIP = 128
N_ATT_HEADS = 12
ATT_HEAD_DIM = 128
ATT_WIDTH = N_ATT_HEADS * ATT_HEAD_DIM
RNN_BLOCKS = 12
RNN_BLOCK_DIM = 128
RNN_WIDTH = RNN_BLOCKS * RNN_BLOCK_DIM
CONV_WIDTH = 4
LRU_C = 8.0
N_MEM_HEADS = 4
MEM_HEAD_DIM = 256
MEM_WIDTH = N_MEM_HEADS * MEM_HEAD_DIM
N_BRANCH = 3
N_EXPERTS = 32
TOP_K = 4
SWIGLU_LIMIT = 7.0
SWIGLU_ALPHA = 1.702
LN_EPS = 1e-5
DN_ALPHA = (2.0 * DEPTH) ** 0.25
NEG_BIG = -1e30

LANES = 128
SUBLANES = 8
QBLOCK = 2 * CHUNK
KBLOCKS = PAST_CHUNKS // 2 + 1
MOE_TM = 512
ROUTE_TM = 512
MOVE_TM = 256


def _cparams(n_axes, vmem_mb=48):
    return pltpu.CompilerParams(dimension_semantics=("arbitrary",) * n_axes,
                                vmem_limit_bytes=vmem_mb * 1024 * 1024)


def _mm_body(a_ref, w_ref, *refs, has_bias, act, cast_a):
    refs = list(refs)
    b_ref = refs.pop(0) if has_bias else None
    o_ref = refs.pop(0)
    if cast_a:
        a_scr = refs.pop(0)

        @pl.when(pl.program_id(1) == 0)
        def _():
            a_scr[...] = a_ref[...].astype(BF16)

        a = a_scr[...]
    else:
        a = a_ref[...]
    acc = jnp.dot(a, w_ref[...], preferred_element_type=F32)
    if has_bias:
        acc = acc + b_ref[...]
    if act == "sigmoid":
        acc = jax.nn.sigmoid(acc)
    o_ref[...] = acc.astype(o_ref.dtype)


def _mm(a, w, *, out_dtype, name, bias=None, act=None, tm=1024, tn=512):
    M, K = a.shape
    N = w.shape[1]
    tm, tn = min(tm, M), min(tn, N)
    assert M % tm == 0 and N % tn == 0 and w.dtype == BF16
    cast_a = a.dtype != BF16
    in_specs = [pl.BlockSpec((tm, K), lambda i, j: (i, 0)),
                pl.BlockSpec((K, tn), lambda i, j: (0, j))]
    args = [a, w]
    if bias is not None:
        in_specs.append(pl.BlockSpec((1, tn), lambda i, j: (0, j)))
        args.append(bias.reshape(1, N).astype(F32))
    return pl.pallas_call(
        functools.partial(_mm_body, has_bias=bias is not None, act=act, cast_a=cast_a),
        grid=(M // tm, N // tn),
        in_specs=in_specs,
        out_specs=pl.BlockSpec((tm, tn), lambda i, j: (i, j)),
        out_shape=jax.ShapeDtypeStruct((M, N), out_dtype),
        scratch_shapes=[pltpu.VMEM((tm, K), BF16)] if cast_a else [],
        compiler_params=_cparams(2),
        name=name,
    )(*args)


def _rglru_body(u_ref, rg_ref, cw_ref, cb_ref, wa_ref, ba_ref, wx_ref, bx_ref, lam_ref, c0_ref, h0_ref,
                hg_ref, hl_ref, ext_scr, a_scr, b_scr, hcar_scr, *, tc):
    c = pl.program_id(1)

    @pl.when(c == 0)
    def _():
        ext_scr[pl.ds(0, SUBLANES), :] = jnp.zeros((SUBLANES, RNN_WIDTH), F32)
        ext_scr[pl.ds(SUBLANES - (CONV_WIDTH - 1), CONV_WIDTH - 1), :] = c0_ref[0]
        hcar_scr[...] = h0_ref[0]

    u = u_ref[...]
    ext_scr[pl.ds(SUBLANES, tc), :] = u
    uc = cb_ref[...]
    for tap in range(CONV_WIDTH):
        off = SUBLANES - (CONV_WIDTH - 1) + tap
        uc = uc + ext_scr[pl.ds(off, tc), :] * cw_ref[pl.ds(tap, 1), :]
    ext_scr[pl.ds(0, SUBLANES), :] = u[tc - SUBLANES:, :]

    lam = lam_ref[...]
    neg_c_sp = -LRU_C * jax.nn.softplus(-lam)
    for n in range(RNN_BLOCKS):
        sl = slice(n * RNN_BLOCK_DIM, (n + 1) * RNN_BLOCK_DIM)
        ucn = uc[:, sl]
        ucb = ucn.astype(BF16)
        r = jax.nn.sigmoid(jnp.dot(ucb, wa_ref[n], preferred_element_type=F32) + ba_ref[:, sl])
        gi = jax.nn.sigmoid(jnp.dot(ucb, wx_ref[n], preferred_element_type=F32) + bx_ref[:, sl])
        log_a = r * neg_c_sp[:, sl]
        a = jnp.exp(log_a)
        a_scr[:, sl] = a
        b_scr[:, sl] = jnp.sqrt(-jnp.tanh(log_a) * (a * a + 1.0)) * (gi * ucn)

    row = lax.broadcasted_iota(jnp.int32, (SUBLANES, RNN_WIDTH), 0)

    def group(g, hprev):
        r0 = pl.multiple_of(g * SUBLANES, SUBLANES)
        a8 = a_scr[pl.ds(r0, SUBLANES), :]
        b8 = b_scr[pl.ds(r0, SUBLANES), :]
        for d in (1, 2, 4):
            a_sh = pltpu.roll(a8, d, 0)
            b_sh = pltpu.roll(b8, d, 0)
            m = row >= d
            b8 = jnp.where(m, a8 * b_sh + b8, b8)
            a8 = jnp.where(m, a8 * a_sh, a8)
        h8 = a8 * hprev + b8
        b_scr[pl.ds(r0, SUBLANES), :] = h8
        return h8[SUBLANES - 1:SUBLANES, :]

    hlast = lax.fori_loop(0, tc // SUBLANES, group, hcar_scr[...])
    hcar_scr[...] = hlast
    hl_ref[0] = hlast
    hg_ref[...] = (b_scr[...] * jax.nn.gelu(rg_ref[...])).astype(BF16)


def _rglru(zr, conv0, h0, conv_w, conv_b, wa, ba, wx, bx, lam, *, n_batch, t_len, name):
    tc = min(512, t_len)
    assert t_len % tc == 0 and tc % SUBLANES == 0
    n_t = t_len // tc
    C = RNN_WIDTH
    row2 = lambda v: v.reshape(1, C).astype(F32)
    const2 = lambda shape: pl.BlockSpec(shape, lambda b, c: (0, 0))
    const3 = lambda shape: pl.BlockSpec(shape, lambda b, c: (0, 0, 0))
    return pl.pallas_call(
        functools.partial(_rglru_body, tc=tc),
        grid=(n_batch, n_t),
        in_specs=[pl.BlockSpec((tc, C), lambda b, c: (b * n_t + c, 0)),
                  pl.BlockSpec((tc, C), lambda b, c: (b * n_t + c, 1)),
                  const2((CONV_WIDTH, C)), const2((1, C)),
                  const3((RNN_BLOCKS, RNN_BLOCK_DIM, RNN_BLOCK_DIM)), const2((1, C)),
                  const3((RNN_BLOCKS, RNN_BLOCK_DIM, RNN_BLOCK_DIM)), const2((1, C)),
                  const2((1, C)),
                  pl.BlockSpec((1, CONV_WIDTH - 1, C), lambda b, c: (b, 0, 0)),
                  pl.BlockSpec((1, 1, C), lambda b, c: (b, 0, 0))],
        out_specs=[pl.BlockSpec((tc, C), lambda b, c: (b * n_t + c, 0)),
                   pl.BlockSpec((1, 1, C), lambda b, c: (b, 0, 0))],
        out_shape=[jax.ShapeDtypeStruct((n_batch * t_len, C), BF16),
                   jax.ShapeDtypeStruct((n_batch, 1, C), F32)],
        scratch_shapes=[pltpu.VMEM((tc + SUBLANES, C), F32), pltpu.VMEM((tc, C), F32),
                        pltpu.VMEM((tc, C), F32), pltpu.VMEM((1, C), F32)],
        compiler_params=_cparams(2),
        name=name,
    )(zr, zr, conv_w.astype(F32), row2(conv_b), wa, row2(ba), wx, row2(bx), row2(lam),
      conv0.astype(F32), h0.reshape(n_batch, 1, C).astype(F32))


def _band_bias(rel_bias):
    qi = jnp.arange(QBLOCK)[:, None]
    kj = jnp.arange(KBLOCKS * QBLOCK)[None, :]
    dist = qi + (KBLOCKS - 1) * QBLOCK - kj
    bias = rel_bias[:, jnp.clip(dist, -REL_CLIP, REL_CLIP) + REL_CLIP].astype(F32)
    chunk_gap = (qi // CHUNK + PAST_CHUNKS) - kj // CHUNK
    in_band = (chunk_gap >= 0) & (chunk_gap <= PAST_CHUNKS)
    return jnp.where(in_band[None], bias, NEG_BIG)


def _band_attn_body(q_ref, k_ref, v_ref, bias_ref, o_ref, *, n_qb):
    scale = ATT_HEAD_DIM ** -0.5

    def qblock(i, _):
        q0 = pl.multiple_of(i * QBLOCK, QBLOCK)
        q = q_ref[pl.ds(q0, QBLOCK), :]
        s_parts, v_parts = [], []
        for j in range(KBLOCKS):
            kb = i - (KBLOCKS - 1) + j
            k0 = pl.multiple_of(jnp.maximum(kb, 0) * QBLOCK, QBLOCK)
            kj = k_ref[pl.ds(k0, QBLOCK), :]
            v_parts.append(v_ref[pl.ds(k0, QBLOCK), :])
            sj = lax.dot_general(q, kj, (((1,), (1,)), ((), ())), preferred_element_type=F32)
            sj = sj * scale + bias_ref[0, :, j * QBLOCK:(j + 1) * QBLOCK]
            s_parts.append(jnp.where(kb >= 0, sj, NEG_BIG))
        s = jnp.concatenate(s_parts, axis=1)
        m = jnp.max(s, axis=1, keepdims=True)
        p = jnp.exp(s - m)
        l = jnp.sum(p, axis=1, keepdims=True)
        pb = p.astype(BF16)
        o = jnp.zeros((QBLOCK, ATT_HEAD_DIM), F32)
        for j in range(KBLOCKS):
            o = o + jnp.dot(pb[:, j * QBLOCK:(j + 1) * QBLOCK], v_parts[j], preferred_element_type=F32)
        o_ref[pl.ds(q0, QBLOCK), :] = (o / l).astype(BF16)
        return 0

    lax.fori_loop(0, n_qb, qblock, 0)


def _band_attn(za, bias, *, n_batch, t_len):
    assert t_len % QBLOCK == 0
    H = N_ATT_HEADS
    blk = lambda off: pl.BlockSpec((t_len, ATT_HEAD_DIM), lambda b, h: (b, off + h))
    return pl.pallas_call(
        functools.partial(_band_attn_body, n_qb=t_len // QBLOCK),
        grid=(n_batch, H),
        in_specs=[blk(0), blk(H), blk(2 * H),
                  pl.BlockSpec((1, QBLOCK, KBLOCKS * QBLOCK), lambda b, h: (h, 0, 0))],
        out_specs=pl.BlockSpec((t_len, ATT_HEAD_DIM), lambda b, h: (b, h)),
        out_shape=jax.ShapeDtypeStruct((n_batch * t_len, ATT_WIDTH), BF16),
        compiler_params=_cparams(2),
        name="band_attn",
    )(za, za, za, bias)


def _cached_attn_body(q_ref, kc_ref, vc_ref, kn_ref, vn_ref, bc_ref, bn_ref, o_ref):
    scale = ATT_HEAD_DIM ** -0.5
    q = q_ref[...]
    dims = (((1,), (1,)), ((), ()))
    sc = lax.dot_general(q, kc_ref[0].astype(BF16), dims, preferred_element_type=F32) * scale + bc_ref[0]
    sn = lax.dot_general(q, kn_ref[...], dims, preferred_element_type=F32) * scale + bn_ref[0]
    m = jnp.maximum(jnp.max(sc, axis=1, keepdims=True), jnp.max(sn, axis=1, keepdims=True))
    pc = jnp.exp(sc - m)
    pn = jnp.exp(sn - m)
    l = jnp.sum(pc, axis=1, keepdims=True) + jnp.sum(pn, axis=1, keepdims=True)
    o = (jnp.dot(pc.astype(BF16), vc_ref[0].astype(BF16), preferred_element_type=F32)
         + jnp.dot(pn.astype(BF16), vn_ref[...], preferred_element_type=F32))
    o_ref[...] = (o / l).astype(BF16)


def _cached_attn(za, cache_k, cache_v, rel_bias, *, n_batch, t_len):
    H = N_ATT_HEADS
    R = cache_k.shape[1]
    dist = jnp.arange(t_len)[:, None] + R - jnp.arange(R + t_len)[None, :]
    bias = rel_bias[:, jnp.clip(dist, -REL_CLIP, REL_CLIP) + REL_CLIP].astype(F32)
    ck = cache_k.reshape(n_batch, R, ATT_WIDTH)
    cv = cache_v.reshape(n_batch, R, ATT_WIDTH)
    new = lambda off: pl.BlockSpec((t_len, ATT_HEAD_DIM), lambda b, h: (b, off + h))
    old = pl.BlockSpec((1, R, ATT_HEAD_DIM), lambda b, h: (b, 0, h))
    return pl.pallas_call(
        _cached_attn_body,
        grid=(n_batch, H),
        in_specs=[new(0), old, old, new(H), new(2 * H),
                  pl.BlockSpec((1, t_len, R), lambda b, h: (h, 0, 0)),
                  pl.BlockSpec((1, t_len, t_len), lambda b, h: (h, 0, 0))],
        out_specs=pl.BlockSpec((t_len, ATT_HEAD_DIM), lambda b, h: (b, h)),
        out_shape=jax.ShapeDtypeStruct((n_batch * t_len, ATT_WIDTH), BF16),
        compiler_params=_cparams(2),
        name="cached_attn",
    )(za, ck, cv, za, za, bias[:, :, :R], bias[:, :, R:])


def _mem_attn_body(q_ref, k_ref, v_ref, o_ref):
    scale = MEM_HEAD_DIM ** -0.5
    q = q_ref[...]
    k = k_ref[...].astype(BF16)
    v = v_ref[...].astype(BF16)
    s = lax.dot_general(q, k, (((1,), (1,)), ((), ())), preferred_element_type=F32) * scale
    m = jnp.max(s, axis=1, keepdims=True)
    p = jnp.exp(s - m)
    l = jnp.sum(p, axis=1, keepdims=True)
    o_ref[...] = (jnp.dot(p.astype(BF16), v, preferred_element_type=F32) / l).astype(BF16)


def _mem_attn(za, mk, mv, k_off, v_off, *, n_batch, t_len, n_mem):
    tq = min(1024, t_len)
    n_q = t_len // tq
    q_off = 3 * ATT_WIDTH // MEM_HEAD_DIM
    return pl.pallas_call(
        _mem_attn_body,
        grid=(n_batch, N_MEM_HEADS, n_q),
        in_specs=[pl.BlockSpec((tq, MEM_HEAD_DIM), lambda b, h, i: (b * n_q + i, q_off + h)),
                  pl.BlockSpec((n_mem, MEM_HEAD_DIM), lambda b, h, i: (b, k_off + h)),
                  pl.BlockSpec((n_mem, MEM_HEAD_DIM), lambda b, h, i: (b, v_off + h))],
        out_specs=pl.BlockSpec((tq, MEM_HEAD_DIM), lambda b, h, i: (b * n_q + i, h)),
        out_shape=jax.ShapeDtypeStruct((n_batch * t_len, MEM_WIDTH), BF16),
        compiler_params=_cparams(3),
        name="mem_attn",
    )(za, mk, mv)


def _merge_body(hg_ref, oa_ref, om_ref, wr_ref, wa_ref, wm_ref, g0_ref, g1_ref, g2_ref, o_ref):
    acc = g0_ref[...].astype(F32) * jnp.dot(hg_ref[...], wr_ref[...], preferred_element_type=F32)
    acc = acc + g1_ref[...].astype(F32) * jnp.dot(oa_ref[...], wa_ref[...], preferred_element_type=F32)
    acc = acc + g2_ref[...].astype(F32) * jnp.dot(om_ref[...], wm_ref[...], preferred_element_type=F32)
    o_ref[...] = acc.astype(BF16)


def _merge(hg, oatt, omem, w_r, w_a, w_m, gate, *, name):
    M = hg.shape[0]
    tm, tn = min(1024, M), 512
    n_j = D_MODEL // tn
    a_spec = lambda k: pl.BlockSpec((tm, k), lambda i, j: (i, 0))
    w_spec = lambda k: pl.BlockSpec((k, tn), lambda i, j: (0, j))
    g_spec = lambda b: pl.BlockSpec((tm, tn), lambda i, j: (i, b * n_j + j))
    return pl.pallas_call(
        _merge_body,
        grid=(M // tm, n_j),
        in_specs=[a_spec(RNN_WIDTH), a_spec(ATT_WIDTH), a_spec(MEM_WIDTH),
                  w_spec(RNN_WIDTH), w_spec(ATT_WIDTH), w_spec(MEM_WIDTH),
                  g_spec(0), g_spec(1), g_spec(2)],
        out_specs=pl.BlockSpec((tm, tn), lambda i, j: (i, j)),
        out_shape=jax.ShapeDtypeStruct((M, D_MODEL), BF16),
        compiler_params=_cparams(2),
        name=name,
    )(hg, oatt, omem, w_r, w_a, w_m, gate, gate, gate)


def _layer_norm_rows(x, g, b):
    mu = jnp.mean(x, axis=-1, keepdims=True)
    xc = x - mu
    var = jnp.mean(xc * xc, axis=-1, keepdims=True)
    return xc * lax.rsqrt(var + LN_EPS) * g + b


def _out_ln_body(m_ref, w_ref, x_ref, g_ref, b_ref, wrh_ref, wrl_ref, br_ref, x1_ref, lg_ref):
    y = DN_ALPHA * x_ref[...] + jnp.dot(m_ref[...], w_ref[...], preferred_element_type=F32)
    x1 = _layer_norm_rows(y, g_ref[...], b_ref[...])
    x1_ref[...] = x1
    hi = x1.astype(BF16)
    lo = (x1 - hi.astype(F32)).astype(BF16)
    lg = jnp.dot(hi, wrh_ref[...], preferred_element_type=F32)
    lg = lg + jnp.dot(hi, wrl_ref[...], preferred_element_type=F32)
    lg = lg + jnp.dot(lo, wrh_ref[...], preferred_element_type=F32)
    lg_ref[...] = lg + br_ref[...]


def _out_ln(mixed, w_out, x, ln_g, ln_b, w_router, b_router, *, name):
    M = mixed.shape[0]
    tm = min(512, M)
    pad = LANES - N_EXPERTS
    wr = jnp.pad(w_router.astype(F32), ((0, 0), (0, pad)))
    wr_hi = wr.astype(BF16)
    wr_lo = (wr - wr_hi.astype(F32)).astype(BF16)
    br = jnp.pad(b_router.astype(F32), (0, pad), constant_values=NEG_BIG).reshape(1, LANES)
    row = lambda n: pl.BlockSpec((tm, n), lambda i: (i, 0))
    const = lambda r, n: pl.BlockSpec((r, n), lambda i: (0, 0))
    return pl.pallas_call(
        _out_ln_body,
        grid=(M // tm,),
        in_specs=[row(D_MODEL), const(D_MODEL, D_MODEL), row(D_MODEL), const(1, D_MODEL), const(1, D_MODEL),
                  const(D_MODEL, LANES), const(D_MODEL, LANES), const(1, LANES)],
        out_specs=[row(D_MODEL), row(LANES)],
        out_shape=[jax.ShapeDtypeStruct((M, D_MODEL), F32), jax.ShapeDtypeStruct((M, LANES), F32)],
        compiler_params=_cparams(1),
        name=name,
    )(mixed, w_out, x, ln_g.reshape(1, D_MODEL), ln_b.reshape(1, D_MODEL), wr_hi, wr_lo, br)


def _route_body(lg_ref, o_ref, cnt_ref, carry_scr, *, tm, n_tok):
    i = pl.program_id(0)

    @pl.when(i == 0)
    def _():
        carry_scr[...] = jnp.zeros((1, LANES), F32)

    lanef = lax.broadcasted_iota(jnp.int32, (tm, LANES), 1).astype(F32)
    work = lg_ref[...]
    es, ms = [], []
    for _ in range(TOP_K):
        m = jnp.max(work, axis=1, keepdims=True)
        e = jnp.min(jnp.where(work == m, lanef, float(LANES)), axis=1, keepdims=True)
        es.append(e)
        ms.append(m)
        work = jnp.where(lanef == e, -jnp.inf, work)
    ex = [jnp.exp(m - ms[0]) for m in ms]
    den = ex[0] + ex[1] + ex[2] + ex[3]
    gates = [v / den for v in ex]

    tok = i * tm + lax.broadcasted_iota(jnp.int32, (tm, LANES), 0)
    sel = jnp.zeros((tm, LANES), F32)
    for e in es:
        sel = sel + (lanef == e).astype(F32)
    sel = jnp.where(tok < n_tok, sel, 0.0)
    r_i = lax.broadcasted_iota(jnp.int32, (tm, tm), 0)
    c_i = lax.broadcasted_iota(jnp.int32, (tm, tm), 1)
    tri = (c_i < r_i).astype(BF16)
    before = jnp.dot(tri, sel.astype(BF16), preferred_element_type=F32) + carry_scr[...]
    ranks = [jnp.sum(jnp.where(lanef == e, before, 0.0), axis=1, keepdims=True) for e in es]
    carry_scr[...] = carry_scr[...] + jnp.sum(sel, axis=0, keepdims=True)
    cnt_ref[...] = carry_scr[...]

    out = jnp.zeros((tm, LANES), F32)
    for k in range(TOP_K):
        out = jnp.where(lanef == float(k), gates[k], out)
        out = jnp.where(lanef == float(TOP_K + k), es[k], out)
        out = jnp.where(lanef == float(2 * TOP_K + k), ranks[k], out)
    o_ref[...] = out


def _route(logits, n_tok):
    n_pad = logits.shape[0]
    tm = ROUTE_TM
    return pl.pallas_call(
        functools.partial(_route_body, tm=tm, n_tok=n_tok),
        grid=(n_pad // tm,),
        in_specs=[pl.BlockSpec((tm, LANES), lambda i: (i, 0))],
        out_specs=[pl.BlockSpec((tm, LANES), lambda i: (i, 0)), pl.BlockSpec((1, LANES), lambda i: (0, 0))],
        out_shape=[jax.ShapeDtypeStruct((n_pad, LANES), F32), jax.ShapeDtypeStruct((1, LANES), F32)],
        scratch_shapes=[pltpu.VMEM((1, LANES), F32)],
        compiler_params=_cparams(1),
        name="route",
    )(logits)


def _row_copy(src, s, dst, d, sem):
    return pltpu.make_async_copy(src.at[pl.ds(s, 1), :], dst.at[pl.ds(d, 1), :], sem)


def _dispatch_body(dest_hbm, x_hbm, *refs, tm, aliased):
    refs = list(refs)
    if aliased:
        refs.pop(0)
    xs_hbm, idx_smem, sem_idx, sem_row = refs
    i = pl.program_id(0)
    idx_copy = pltpu.make_async_copy(dest_hbm.at[i], idx_smem, sem_idx)
    idx_copy.start()
    idx_copy.wait()

    def issue(t, _):
        for k in range(TOP_K):
            _row_copy(x_hbm, i * tm + t, xs_hbm, idx_smem[t * TOP_K + k], sem_row).start()
        return 0

    lax.fori_loop(0, tm, issue, 0)

    def drain(t, _):
        for k in range(TOP_K):
            _row_copy(x_hbm, 0, xs_hbm, 0, sem_row).wait()
        return 0

    lax.fori_loop(0, tm, drain, 0)


def _dispatch(dest, x1, xs_prev, n_rows, *, name):
    n_tok = x1.shape[0]
    tm = min(MOVE_TM, n_tok)
    assert n_tok % tm == 0
    aliased = xs_prev is not None
    any_spec = pl.BlockSpec(memory_space=pl.ANY)
    args = [dest.reshape(n_tok // tm, tm * TOP_K), x1] + ([xs_prev] if aliased else [])
    return pl.pallas_call(
        functools.partial(_dispatch_body, tm=tm, aliased=aliased),
        grid=(n_tok // tm,),
        in_specs=[any_spec] * len(args),
        out_specs=any_spec,
        out_shape=jax.ShapeDtypeStruct((n_rows, D_MODEL), F32),
        scratch_shapes=[pltpu.SMEM((tm * TOP_K,), jnp.int32), pltpu.SemaphoreType.DMA(()),
                        pltpu.SemaphoreType.DMA(())],
        input_output_aliases={2: 0} if aliased else {},
        compiler_params=pltpu.CompilerParams(dimension_semantics=("arbitrary",), has_side_effects=True),
        name=name,
    )(*args)


def _zero_pad_body(pos_ref, n_ref, z_hbm, xs_in, xs_hbm, sem, *, n_bits):
    del xs_in

    def expert(e, _):
        pos, n = pos_ref[e], n_ref[e]
        head = (-pos) & (SUBLANES - 1)
        for r in range(SUBLANES - 1):
            @pl.when(r < head)
            def _():
                cp = _row_copy(z_hbm, 0, xs_hbm, pos + r, sem)
                cp.start()
                cp.wait()

        pos8, n8 = pos + head, n - head
        for bit in range(int(math.log2(SUBLANES)), n_bits):
            size = 1 << bit
            start = pl.multiple_of(pos8 + (n8 & (size - 1)), SUBLANES)

            @pl.when(((n8 >> bit) & 1) == 1)
            def _():
                cp = pltpu.make_async_copy(z_hbm.at[pl.ds(0, size), :], xs_hbm.at[pl.ds(start, size), :], sem)
                cp.start()
                cp.wait()
        return 0

    lax.fori_loop(0, N_EXPERTS, expert, 0)


def _zero_pad(xs, pad_pos, pad_n):
    n_bits = int(math.log2(MOE_TM))
    any_spec = pl.BlockSpec(memory_space=pl.ANY)
    return pl.pallas_call(
        functools.partial(_zero_pad_body, n_bits=n_bits),
        grid_spec=pltpu.PrefetchScalarGridSpec(
            num_scalar_prefetch=2, grid=(1,),
            in_specs=[any_spec, any_spec], out_specs=any_spec,
            scratch_shapes=[pltpu.SemaphoreType.DMA(())]),
        out_shape=jax.ShapeDtypeStruct(xs.shape, F32),
        input_output_aliases={3: 0},
        compiler_params=pltpu.CompilerParams(dimension_semantics=("arbitrary",), has_side_effects=True),
        name="moe_zero_pad",
    )(pad_pos, pad_n, jnp.zeros((MOE_TM, D_MODEL), F32), xs)


def _ffn_up_body(be_ref, ib_ref, ob_ref, ok_ref, x_ref, wg_ref, bg_ref, wu_ref, bu_ref, h_ref):
    del be_ref, ib_ref, ob_ref

    @pl.when(ok_ref[pl.program_id(1)] == 1)
    def _():
        x = x_ref[...].astype(BF16)
        g = jnp.minimum(jnp.dot(x, wg_ref[0], preferred_element_type=F32) + bg_ref[0], SWIGLU_LIMIT)
        u = jnp.clip(jnp.dot(x, wu_ref[0], preferred_element_type=F32) + bu_ref[0], -SWIGLU_LIMIT, SWIGLU_LIMIT)
        h_ref[...] = ((u + 1.0) * (g * jax.nn.sigmoid(SWIGLU_ALPHA * g))).astype(BF16)

    @pl.when(ok_ref[pl.program_id(1)] == 0)
    def _():
        h_ref[...] = jnp.zeros(h_ref.shape, BF16)


def _ffn_down_body(be_ref, ib_ref, ob_ref, ok_ref, h_ref, wd_ref, bd_ref, y_ref):
    del be_ref, ib_ref, ob_ref

    @pl.when(ok_ref[pl.program_id(0)] == 1)
    def _():
        y_ref[...] = jnp.dot(h_ref[...], wd_ref[0], preferred_element_type=F32) + bd_ref[0]

    @pl.when(ok_ref[pl.program_id(0)] == 0)
    def _():
        y_ref[...] = jnp.zeros(y_ref.shape, F32)


def _ffn(xs, tables, wg, bg, wu, bu, wd, bd, n_blocks):
    tm = MOE_TM
    d_ff = wg.shape[2]
    fh = d_ff // 2
    rows_out = (n_blocks + 1) * tm
    h = pl.pallas_call(
        _ffn_up_body,
        grid_spec=pltpu.PrefetchScalarGridSpec(
            num_scalar_prefetch=4, grid=(2, n_blocks),
            in_specs=[pl.BlockSpec((tm, D_MODEL), lambda f, i, be, ib, ob, ok: (ib[i], 0)),
                      pl.BlockSpec((1, D_MODEL, fh), lambda f, i, be, ib, ob, ok: (be[i], 0, f)),
                      pl.BlockSpec((1, 1, fh), lambda f, i, be, ib, ob, ok: (be[i], 0, f)),
                      pl.BlockSpec((1, D_MODEL, fh), lambda f, i, be, ib, ob, ok: (be[i], 0, f)),
                      pl.BlockSpec((1, 1, fh), lambda f, i, be, ib, ob, ok: (be[i], 0, f))],
            out_specs=pl.BlockSpec((tm, fh), lambda f, i, be, ib, ob, ok: (ob[i], f))),
        out_shape=jax.ShapeDtypeStruct((rows_out, d_ff), BF16),
        compiler_params=_cparams(2),
        name="moe_ffn_up",
    )(*tables, xs, wg, bg.reshape(N_EXPERTS, 1, d_ff), wu, bu.reshape(N_EXPERTS, 1, d_ff))
    return pl.pallas_call(
        _ffn_down_body,
        grid_spec=pltpu.PrefetchScalarGridSpec(
            num_scalar_prefetch=4, grid=(n_blocks,),
            in_specs=[pl.BlockSpec((tm, d_ff), lambda i, be, ib, ob, ok: (ib[i], 0)),
                      pl.BlockSpec((1, d_ff, D_MODEL), lambda i, be, ib, ob, ok: (be[i], 0, 0)),
                      pl.BlockSpec((1, 1, D_MODEL), lambda i, be, ib, ob, ok: (be[i], 0, 0))],
            out_specs=pl.BlockSpec((tm, D_MODEL), lambda i, be, ib, ob, ok: (ob[i], 0))),
        out_shape=jax.ShapeDtypeStruct((rows_out, D_MODEL), F32),
        compiler_params=_cparams(1),
        name="moe_ffn_down",
    )(*tables, h, wd, bd.reshape(N_EXPERTS, 1, D_MODEL))


def _combine_body(dest_hbm, ys_hbm, slab_ref, x1_ref, g_ref, b_ref, o_ref, idx_smem, rows_scr, sem_idx, sem_row, *, tm, blk0):
    i = pl.program_id(0)
    idx_copy = pltpu.make_async_copy(dest_hbm.at[blk0 + i], idx_smem, sem_idx)
    idx_copy.start()
    idx_copy.wait()

    def issue(t, _):
        for k in range(TOP_K):
            pltpu.make_async_copy(ys_hbm.at[pl.ds(idx_smem[t * TOP_K + k], 1), :],
                                  rows_scr.at[k, pl.ds(t, 1), :], sem_row).start()
        return 0

    lax.fori_loop(0, tm, issue, 0)

    def drain(t, _):
        for k in range(TOP_K):
            pltpu.make_async_copy(ys_hbm.at[pl.ds(0, 1), :], rows_scr.at[k, pl.ds(0, 1), :], sem_row).wait()
        return 0

    lax.fori_loop(0, tm, drain, 0)

    acc = DN_ALPHA * x1_ref[...]
    slab = slab_ref[...]
    for k in range(TOP_K):
        acc = acc + slab[:, k:k + 1] * rows_scr[k]
    o_ref[...] = _layer_norm_rows(acc, g_ref[...], b_ref[...])


def _combine(dest_tiles, ys, slab, x1, ln_g, ln_b, *, tok0, name):
    n_tok = x1.shape[0]
    tm = min(MOVE_TM, n_tok)
    assert n_tok % tm == 0 and tok0 % tm == 0
    blk0 = tok0 // tm
    any_spec = pl.BlockSpec(memory_space=pl.ANY)
    return pl.pallas_call(
        functools.partial(_combine_body, tm=tm, blk0=blk0),
        grid=(n_tok // tm,),
        in_specs=[any_spec, any_spec,
                  pl.BlockSpec((tm, LANES), lambda i: (blk0 + i, 0)),
                  pl.BlockSpec((tm, D_MODEL), lambda i: (i, 0)),
                  pl.BlockSpec((1, D_MODEL), lambda i: (0, 0)),
                  pl.BlockSpec((1, D_MODEL), lambda i: (0, 0))],
        out_specs=pl.BlockSpec((tm, D_MODEL), lambda i: (i, 0)),
        out_shape=jax.ShapeDtypeStruct((n_tok, D_MODEL), F32),
        scratch_shapes=[pltpu.SMEM((tm * TOP_K,), jnp.int32), pltpu.VMEM((TOP_K, tm, D_MODEL), F32),
                        pltpu.SemaphoreType.DMA(()), pltpu.SemaphoreType.DMA(())],
        compiler_params=_cparams(1),
        name=name,
    )(dest_tiles.reshape(-1, tm * TOP_K), ys, slab, x1, ln_g.reshape(1, D_MODEL), ln_b.reshape(1, D_MODEL))


def _moe(x1_list, logits_list, p):
    n_toks = [x.shape[0] for x in x1_list]
    n_tok = sum(n_toks)
    n_pad = -(-n_tok // ROUTE_TM) * ROUTE_TM
    logits = jnp.concatenate(logits_list + [jnp.zeros((n_pad - n_tok, LANES), F32)], axis=0)
    slab, counts = _route(logits, n_tok)

    tm = MOE_TM
    n_assign = n_tok * TOP_K
    n_blocks = -(-(n_assign + N_EXPERTS * (tm - 1)) // tm)
    counts = counts[0, :N_EXPERTS].astype(jnp.int32)
    padded = (counts + tm - 1) // tm * tm
    pad_end = jnp.cumsum(padded)
    pad_start = pad_end - padded
    blk = jnp.arange(n_blocks, dtype=jnp.int32)
    used = blk * tm < pad_end[-1]
    block_e = jnp.minimum(jnp.searchsorted(pad_end, blk * tm, side="right"), N_EXPERTS - 1).astype(jnp.int32)
    last_used = jnp.maximum(pad_end[-1] // tm - 1, 0).astype(jnp.int32)
    in_blk = jnp.where(used, blk, last_used)
    block_e = jnp.where(used, block_e, block_e[last_used])
    out_blk = jnp.where(used, blk, n_blocks)
    tables = (block_e, in_blk.astype(jnp.int32), out_blk.astype(jnp.int32), used.astype(jnp.int32))

    top_e = slab[:n_tok, TOP_K:2 * TOP_K].astype(jnp.int32)
    rank = slab[:n_tok, 2 * TOP_K:3 * TOP_K].astype(jnp.int32)
    dest = pad_start[top_e].astype(jnp.int32) + rank

    n_rows = n_blocks * tm
    xs, t0 = None, 0
    for g, x1 in enumerate(x1_list):
        xs = _dispatch(dest[t0:t0 + n_toks[g]], x1, xs, n_rows, name=f"moe_dispatch_{g}")
        t0 += n_toks[g]
    xs = _zero_pad(xs, (pad_start + counts).astype(jnp.int32), (padded - counts).astype(jnp.int32))

    ys = _ffn(xs, tables, p["w_moe_gate"], p["b_moe_gate"], p["w_moe_up"], p["b_moe_up"],
              p["w_moe_down"], p["b_moe_down"], n_blocks)

    outs, t0 = [], 0
    dest_pad = jnp.concatenate([dest, jnp.zeros((n_pad - n_tok, TOP_K), jnp.int32)], axis=0)
    for g, x1 in enumerate(x1_list):
        outs.append(_combine(dest_pad, ys, slab, x1, p["ln2_g"], p["ln2_b"], tok0=t0, name=f"moe_combine_{g}"))
        t0 += n_toks[g]
    return outs


def _mixers(x, conv0, h0, attend, mem_kv, p):
    B, T, _ = x.shape
    M = B * T
    xf = x.reshape(M, D_MODEL)
    zr = _mm(xf, p["w_in_rnn"], out_dtype=F32, name="in_proj_rnn")
    za = _mm(xf, p["w_in_att"], out_dtype=BF16, name="in_proj_att")
    gate = _mm(xf, p["w_gate"], out_dtype=BF16, bias=p["b_gate"], act="sigmoid", name="gate_proj")
    hg, h_last = _rglru(zr, conv0, h0, p["conv_w"], p["conv_b"], p["lru_wa"], p["lru_ba"], p["lru_wx"],
                        p["lru_bx"], p["lru_lambda"], n_batch=B, t_len=T, name="rglru")
    oatt = attend(za)
    mk, mv, k_off, v_off, n_mem = mem_kv
    omem = _mem_attn(za, mk, mv, k_off, v_off, n_batch=B, t_len=T, n_mem=n_mem)
    mixed = _merge(hg, oatt, omem, p["w_proj_rnn"], p["w_proj_att"], p["w_proj_mem"], gate, name="merge")
    x1, logits = _out_ln(mixed, p["w_out"], xf, p["ln1_g"], p["ln1_b"], p["w_router"], p["b_router"], name="out_ln")
    conv_state = zr.reshape(B, T, 2 * RNN_WIDTH)[:, T - (CONV_WIDTH - 1):, :RNN_WIDTH]
    return x1, logits, conv_state, h_last.reshape(B, RNN_WIDTH)


def kernel(x_prompt, x_sample, cache_att_k, cache_att_v, cache_mem_k, cache_mem_v, state_conv, state_rglru, mem_prompt, w_in, conv_w, conv_b, lru_wa, lru_ba, lru_wx, lru_bx, lru_lambda, rel_bias, w_mem_kv, w_proj_rnn, w_proj_att, w_proj_mem, w_gate, b_gate, w_out, ln1_g, ln1_b, w_router, b_router, w_moe_gate, b_moe_gate, w_moe_up, b_moe_up, w_moe_down, b_moe_down, ln2_g, ln2_b):
    assert w_in.shape[0] == DEPTH == 1
    B, T, _ = x_prompt.shape
    Bs, Ts, _ = x_sample.shape
    n_mem = mem_prompt.shape[1]
    keep = min(PAST_CHUNKS * CHUNK, T)
    bf = lambda w: w[0].astype(BF16)
    w_in_b = bf(w_in)
    p = dict(
        w_in_rnn=w_in_b[:, :2 * RNN_WIDTH], w_in_att=w_in_b[:, 2 * RNN_WIDTH:],
        w_gate=bf(w_gate), b_gate=b_gate[0], conv_w=conv_w[0], conv_b=conv_b[0],
        lru_wa=bf(lru_wa), lru_ba=lru_ba[0], lru_wx=bf(lru_wx), lru_bx=lru_bx[0], lru_lambda=lru_lambda[0],
        w_proj_rnn=bf(w_proj_rnn), w_proj_att=bf(w_proj_att), w_proj_mem=bf(w_proj_mem), w_out=bf(w_out),
        ln1_g=ln1_g[0], ln1_b=ln1_b[0], w_router=w_router[0], b_router=b_router[0],
        w_moe_gate=bf(w_moe_gate), b_moe_gate=b_moe_gate[0], w_moe_up=bf(w_moe_up), b_moe_up=b_moe_up[0],
        w_moe_down=bf(w_moe_down), b_moe_down=b_moe_down[0], ln2_g=ln2_g[0], ln2_b=ln2_b[0])

    mem_kv = _mm(mem_prompt.reshape(B * n_mem, D_MODEL), bf(w_mem_kv), out_dtype=F32, name="mem_kv")
    band_bias = _band_bias(rel_bias[0])
    x1_p, lg_p, conv_p, rglru_p = _mixers(
        x_prompt, jnp.zeros((B, CONV_WIDTH - 1, RNN_WIDTH), F32), jnp.zeros((B, RNN_WIDTH), F32),
        functools.partial(_band_attn, bias=band_bias, n_batch=B, t_len=T),
        (mem_kv, mem_kv, 0, N_MEM_HEADS, n_mem), p)
    x_tail = x_prompt[:, T - keep:].reshape(B * keep, D_MODEL)
    kv_tail = _mm(x_tail, w_in_b[:, 2 * RNN_WIDTH + ATT_WIDTH:2 * RNN_WIDTH + 3 * ATT_WIDTH], out_dtype=F32, name="kv_tail")
    att_k_p = kv_tail[:, :ATT_WIDTH].reshape(1, B, keep, N_ATT_HEADS, ATT_HEAD_DIM)
    att_v_p = kv_tail[:, ATT_WIDTH:].reshape(1, B, keep, N_ATT_HEADS, ATT_HEAD_DIM)
    mem_k_p = mem_kv[:, :MEM_WIDTH].reshape(1, B, n_mem, N_MEM_HEADS, MEM_HEAD_DIM)
    mem_v_p = mem_kv[:, MEM_WIDTH:].reshape(1, B, n_mem, N_MEM_HEADS, MEM_HEAD_DIM)

    x1_s, lg_s, conv_s, rglru_s = _mixers(
        x_sample, state_conv[0], state_rglru[0],
        functools.partial(_cached_attn, cache_k=cache_att_k[0], cache_v=cache_att_v[0], rel_bias=rel_bias[0],
                          n_batch=Bs, t_len=Ts),
        (cache_mem_k[0].reshape(Bs * n_mem, MEM_WIDTH), cache_mem_v[0].reshape(Bs * n_mem, MEM_WIDTH), 0, 0, n_mem), p)
    xs_flat = x_sample.reshape(Bs * Ts, D_MODEL)
    kv_new = _mm(xs_flat, w_in_b[:, 2 * RNN_WIDTH + ATT_WIDTH:2 * RNN_WIDTH + 3 * ATT_WIDTH], out_dtype=F32, name="kv_new")
    att_k_s = kv_new[:, :ATT_WIDTH].reshape(1, Bs, Ts, N_ATT_HEADS, ATT_HEAD_DIM)
    att_v_s = kv_new[:, ATT_WIDTH:].reshape(1, Bs, Ts, N_ATT_HEADS, ATT_HEAD_DIM)

    y_p, y_s = _moe([x1_p, x1_s], [lg_p, lg_s], p)
    return (y_p.reshape(B, T, D_MODEL), y_s.reshape(Bs, Ts, D_MODEL), att_k_p, att_v_p, mem_k_p, mem_v_p,
            conv_p[None], rglru_p[None], att_k_s, att_v_s, conv_s[None], rglru_s[None])
```
